```python
import math
import jax, jax.numpy as jnp
from jax import lax
import numpy as np

D_MODEL = 2048
BATCH = 4
SEQ = 4096
DEPTH = 2

SB_HEADS = 8
SB_HEAD_DIM = 128
SB_WIDTH = SB_HEADS * SB_HEAD_DIM
Q_BLOCK = 128
HG_HEADS = 8
HG_DK = 128
HG_DV = 128
HG_WIDTH = HG_HEADS * HG_DK
HG_CHUNK = 64
CONV_CH = 1024
CONV_WIDTH = 31
N_BRANCH = 3
IN_WIDTH = 3 * SB_WIDTH + 4 * HG_WIDTH + 2 * CONV_CH + N_BRANCH * D_MODEL
PEER_HEADS = 8
PEER_NKEYS = 128
PEER_EXPERTS = PEER_NKEYS * PEER_NKEYS
PEER_QDIM = 256
PEER_HALF = PEER_QDIM // 2
PEER_TOPK = 16
PEER_TOKEN_BLOCK = 128
EPS = 1e-6

kernel_name = "hybrid_sb_hgrn2_conformer_peer"


def _split_points():
    widths = [SB_WIDTH] * 3 + [HG_WIDTH] * 4 + [CONV_CH] * 2 + [D_MODEL] * N_BRANCH
    pts, acc = [], 0
    for w in widths[:-1]:
        acc += w
        pts.append(acc)
    return pts


def rmsnorm(x, w):
    xf = x.astype(jnp.float32)
    y = xf * lax.rsqrt(jnp.mean(xf * xf, axis=-1, keepdims=True) + EPS)
    return (y * w.astype(jnp.float32)).astype(x.dtype)


def _heads(a, n_heads):
    B, S, W = a.shape
    return a.reshape(B, S, n_heads, W // n_heads).transpose(0, 2, 1, 3)


def _merge_heads(a):
    B, H, S, Dh = a.shape
    return a.transpose(0, 2, 1, 3).reshape(B, S, H * Dh)


def stick_breaking_attention(q, k, v):
    S = q.shape[2]
    scale = SB_HEAD_DIM ** -0.5
    outs = []
    for blk in range(S // Q_BLOCK):
        t0 = blk * Q_BLOCK
        kv_len = t0 + Q_BLOCK
        qb = q[:, :, t0:kv_len]
        kb = k[:, :, :kv_len]
        vb = v[:, :, :kv_len]
        z = jnp.einsum('bhtd,bhsd->bhts', qb, kb).astype(jnp.float32) * scale
        t_idx = t0 + jnp.arange(Q_BLOCK)[:, None]
        s_idx = jnp.arange(kv_len)[None, :]
        causal = s_idx < t_idx
        log_om = jnp.where(causal, jax.nn.log_sigmoid(-z), 0.0)
        log_rest = lax.cumsum(log_om, axis=3, reverse=True) - log_om
        w = jnp.where(causal, jnp.exp(jax.nn.log_sigmoid(z) + log_rest), 0.0)
        outs.append(jnp.einsum('bhts,bhsd->bhtd', w.astype(vb.dtype), vb))
    return jnp.concatenate(outs, axis=2)


def hgrn2_chunked(q, log_f, k, i):
    B, H, S, DK = q.shape
    DV = i.shape[-1]
    n = S // HG_CHUNK

    def to_chunks(a):
        return jnp.moveaxis(a.reshape(B, H, n, HG_CHUNK, a.shape[-1]), 2, 0)

    mask = jnp.tril(jnp.ones((HG_CHUNK, HG_CHUNK), dtype=bool))

    def step(state, inp):
        q_, lf, k_, i_ = inp
        b = jnp.cumsum(lf, axis=2)
        diff = b[:, :, :, None, :] - b[:, :, None, :, :]
        decay = jnp.exp(jnp.where(mask[:, :, None], diff, -jnp.inf))
        scores = jnp.einsum('bhtd,bhsd,bhtsd->bhts', q_, k_, decay)
        o = (jnp.einsum('bhts,bhsv->bhtv', scores, i_)
             + jnp.einsum('bhtd,bhdv->bhtv', q_ * jnp.exp(b), state))
        b_last = b[:, :, -1:, :]
        new_state = (jnp.swapaxes(jnp.exp(b_last), 2, 3) * state
                     + jnp.einsum('bhsd,bhsv->bhdv', k_ * jnp.exp(b_last - b), i_))
        return new_state, o

    state0 = jnp.zeros((B, H, DK, DV), jnp.float32)
    _, oc = lax.scan(step, state0, (to_chunks(q), to_chunks(log_f), to_chunks(k), to_chunks(i)))
    return jnp.moveaxis(oc, 0, 2).reshape(B, H, S, DV)


def conformer_conv(a, b, conv_w, conv_b, ln_w, ln_b):
    h = a * jax.nn.sigmoid(b)
    h = lax.conv_general_dilated(h, conv_w, window_strides=(1,), padding=[(CONV_WIDTH - 1, 0)],
                                 dimension_numbers=('NWC', 'WIO', 'NWC'),
                                 feature_group_count=CONV_CH) + conv_b
    hf = h.astype(jnp.float32)
    mu = jnp.mean(hf, axis=-1, keepdims=True)
    var = jnp.mean(jnp.square(hf - mu), axis=-1, keepdims=True)
    hn = (hf - mu) * lax.rsqrt(var + EPS) * ln_w.astype(jnp.float32) + ln_b.astype(jnp.float32)
    return jax.nn.silu(hn).astype(a.dtype)


def hgrn_lower_bounds(p):
    c = jnp.cumsum(jax.nn.softmax(p.astype(jnp.float32), axis=0), axis=0)
    return c - c[0:1]


def mixer_block(x, norm_w, w_in, sb_qn, sb_kn, lb, hg_norm_w, conv_w, conv_b, conv_ln_w,
                conv_ln_b, w_br_sb, w_br_hg, w_br_conv, w_o):
    h = rmsnorm(x, norm_w)
    proj = h @ w_in
    (q_sb, k_sb, v_sb, q_hg, f_hg, i_hg, g_hg, a_c, b_c,
     gate_sb, gate_hg, gate_c) = jnp.split(proj, _split_points(), axis=-1)

    qa = rmsnorm(_heads(q_sb, SB_HEADS), sb_qn)
    ka = rmsnorm(_heads(k_sb, SB_HEADS), sb_kn)
    va = _heads(v_sb, SB_HEADS)
    y_sb = _merge_heads(stick_breaking_attention(qa, ka, va)) @ w_br_sb

    lbh = lb.reshape(HG_HEADS, 1, HG_DK)
    f_logit = _heads(f_hg, HG_HEADS).astype(jnp.float32)
    log_f = jnp.logaddexp(jnp.log(lbh), jnp.log1p(-lbh) + jax.nn.log_sigmoid(f_logit))
    k_h = -jnp.expm1(log_f)
    o_h = hgrn2_chunked(_heads(q_hg, HG_HEADS).astype(jnp.float32), log_f, k_h,
                        _heads(i_hg, HG_HEADS).astype(jnp.float32))
    o_h = rmsnorm(o_h, hg_norm_w) * jax.nn.sigmoid(_heads(g_hg, HG_HEADS).astype(jnp.float32))
    y_hg = _merge_heads(o_h).astype(x.dtype) @ w_br_hg

    y_c = conformer_conv(a_c, b_c, conv_w, conv_b, conv_ln_w, conv_ln_b) @ w_br_conv

    merged = (jax.nn.sigmoid(gate_sb) * y_sb + jax.nn.sigmoid(gate_hg) * y_hg
              + jax.nn.sigmoid(gate_c) * y_c)
    return x + merged @ w_o


def peer_block(x, norm_w, w_query, keys1, keys2, u, v):
    B, S, D = x.shape
    N = B * S
    h = rmsnorm(x, norm_w).reshape(N, D)
    q = (h @ w_query).reshape(N, PEER_HEADS, PEER_QDIM)
    s1 = jnp.einsum('nhd,hkd->nhk', q[..., :PEER_HALF], keys1).astype(jnp.float32)
    s2 = jnp.einsum('nhd,hkd->nhk', q[..., PEER_HALF:], keys2).astype(jnp.float32)
    v1, i1 = lax.top_k(s1, PEER_TOPK)
    v2, i2 = lax.top_k(s2, PEER_TOPK)
    cand_s = (v1[..., :, None] + v2[..., None, :]).reshape(N, PEER_HEADS, PEER_TOPK * PEER_TOPK)
    cand_e = (i1[..., :, None] * PEER_NKEYS + i2[..., None, :]).reshape(N, PEER_HEADS, PEER_TOPK * PEER_TOPK)
    top_s, pos = lax.top_k(cand_s, PEER_TOPK)
    experts = jnp.take_along_axis(cand_e, pos, axis=-1)
    gates = jax.nn.softmax(top_s, axis=-1)

    nb = N // PEER_TOKEN_BLOCK

    def expert_block(args):
        hb, eb, gb = args
        ue = jnp.take(u, eb, axis=0)
        act = jax.nn.gelu(jnp.einsum('td,thkd->thk', hb, ue).astype(jnp.float32), approximate=False)
        ve = jnp.take(v, eb, axis=0)
        return jnp.einsum('thk,thkd->td', (gb * act).astype(ve.dtype), ve)

    y = lax.map(expert_block, (h.reshape(nb, PEER_TOKEN_BLOCK, D),
                               experts.reshape(nb, PEER_TOKEN_BLOCK, PEER_HEADS, PEER_TOPK),
                               gates.reshape(nb, PEER_TOKEN_BLOCK, PEER_HEADS, PEER_TOPK)))
    return x + y.reshape(B, S, D).astype(x.dtype)


def setup_inputs(seed: int = 0) -> dict:
    key = jax.random.key(seed)
    ks = jax.random.split(key, 24)

    def nrm(k, shape, scale):
        return jax.random.normal(k, shape, jnp.float32) * scale

    def gain(k, shape):
        return 1.0 + 0.02 * jax.random.normal(k, shape, jnp.float32)

    L = DEPTH
    return {
        "x": nrm(ks[0], (BATCH, SEQ, D_MODEL), 1.0),
        "attn_norm_w": gain(ks[1], (L, D_MODEL)),
        "w_in": nrm(ks[2], (L, D_MODEL, IN_WIDTH), D_MODEL ** -0.5),
        "sb_q_norm_w": gain(ks[3], (L, SB_HEAD_DIM)),
        "sb_k_norm_w": gain(ks[4], (L, SB_HEAD_DIM)),
        "hg_lower_bounds": nrm(ks[5], (L, HG_WIDTH), 0.1),
        "hg_norm_w": gain(ks[6], (L, HG_DV)),
        "conv_w": nrm(ks[7], (L, CONV_WIDTH, 1, CONV_CH), CONV_WIDTH ** -0.5),
        "conv_b": nrm(ks[8], (L, CONV_CH), 0.02),
        "conv_ln_w": gain(ks[9], (L, CONV_CH)),
        "conv_ln_b": nrm(ks[10], (L, CONV_CH), 0.02),
        "w_br_sb": nrm(ks[11], (L, SB_WIDTH, D_MODEL), SB_WIDTH ** -0.5),
        "w_br_hg": nrm(ks[12], (L, HG_WIDTH, D_MODEL), HG_WIDTH ** -0.5),
        "w_br_conv": nrm(ks[13], (L, CONV_CH, D_MODEL), CONV_CH ** -0.5),
        "w_o": nrm(ks[14], (L, D_MODEL, D_MODEL), D_MODEL ** -0.5),
        "ffn_norm_w": gain(ks[15], (L, D_MODEL)),
        "peer_w_query": nrm(ks[16], (L, D_MODEL, PEER_HEADS * PEER_QDIM), D_MODEL ** -0.5),
        "peer_keys1": nrm(ks[17], (L, PEER_HEADS, PEER_NKEYS, PEER_HALF), PEER_HALF ** -0.5),
        "peer_keys2": nrm(ks[18], (L, PEER_HEADS, PEER_NKEYS, PEER_HALF), PEER_HALF ** -0.5),
        "peer_u": nrm(ks[19], (L, PEER_EXPERTS, D_MODEL), D_MODEL ** -0.5),
        "peer_v": nrm(ks[20], (L, PEER_EXPERTS, D_MODEL), PEER_HEADS ** -0.5),
    }


def reference(x, attn_norm_w, w_in, sb_q_norm_w, sb_k_norm_w, hg_lower_bounds, hg_norm_w,
              conv_w, conv_b, conv_ln_w, conv_ln_b, w_br_sb, w_br_hg, w_br_conv, w_o,
              ffn_norm_w, peer_w_query, peer_keys1, peer_keys2, peer_u, peer_v):
    lbs = hgrn_lower_bounds(hg_lower_bounds)
    for l in range(DEPTH):
        x = mixer_block(x, attn_norm_w[l], w_in[l], sb_q_norm_w[l], sb_k_norm_w[l], lbs[l],
                        hg_norm_w[l], conv_w[l], conv_b[l], conv_ln_w[l], conv_ln_b[l],
                        w_br_sb[l], w_br_hg[l], w_br_conv[l], w_o[l])
        x = peer_block(x, ffn_norm_w[l], peer_w_query[l], peer_keys1[l], peer_keys2[l],
                       peer_u[l], peer_v[l])
    return x
```

```python
import functools

import jax
import jax.numpy as jnp
from jax import lax
from jax.experimental import pallas as pl
from jax.experimental.pallas import tpu as pltpu

F32 = jnp.float32
BF16 = jnp.bfloat16
EPS = 1e-6

LANES = 128
HEADS = 8
SB_WIDTH = HEADS * LANES
HG_WIDTH = HEADS * LANES
CONV_CH = 1024
CONV_TAPS = 31
CONV_HALO = 32
HG_CHUNK = 64
HG_SUB = 16
PEER_TOPK = 16
PEER_NKEYS = 128
VMEM_LIMIT = 56 * 1024 * 1024


def _params(sem):
    return pltpu.CompilerParams(dimension_semantics=sem, vmem_limit_bytes=VMEM_LIMIT)


def _rms_rows(x, w):
    ms = jnp.mean(x * x, axis=-1, keepdims=True)
    return x * lax.rsqrt(ms + EPS) * w


def _log_sigmoid(z):
    return jnp.minimum(z, 0.0) - jnp.log(1.0 + jnp.exp(-jnp.abs(z)))


def _sigmoid(z):
    return 1.0 / (1.0 + jnp.exp(-z))


def _norm_matmul_body(x_ref, nw_ref, w_ref, o_ref, h_ref, *, chunk):
    @pl.when(pl.program_id(1) == 0)
    def _():
        def body(c, carry):
            r = pl.ds(pl.multiple_of(c * chunk, chunk), chunk)
            h_ref[r, :] = _rms_rows(x_ref[r, :], nw_ref[...]).astype(BF16)
            return carry
        lax.fori_loop(0, x_ref.shape[0] // chunk, body, 0)

    o_ref[...] = jnp.dot(h_ref[...], w_ref[...], preferred_element_type=F32).astype(o_ref.dtype)


def norm_matmul(x, nw, w, *, tm, tn, out_dtype=F32):
    n, d = x.shape
    m = w.shape[1]
    return pl.pallas_call(
        functools.partial(_norm_matmul_body, chunk=min(tm, 256)),
        grid=(n // tm, m // tn),
        in_specs=[
            pl.BlockSpec((tm, d), lambda i, j: (i, 0)),
            pl.BlockSpec((1, d), lambda i, j: (0, 0)),
            pl.BlockSpec((d, tn), lambda i, j: (0, j)),
        ],
        out_specs=pl.BlockSpec((tm, tn), lambda i, j: (i, j)),
        out_shape=jax.ShapeDtypeStruct((n, m), out_dtype),
        scratch_shapes=[pltpu.VMEM((tm, d), BF16)],
        compiler_params=_params(("parallel", "arbitrary")),
    )(x, nw.reshape(1, d), w)


def _sb_attn_body(q_ref, k_ref, v_ref, qn_ref, kn_ref, u2_ref, o_ref, kn_s, v_s, *, tq, chunk):
    qi = pl.program_id(2)
    seq = k_ref.shape[0]
    scale = LANES ** -0.5

    @pl.when(qi == 0)
    def _():
        def body(c, carry):
            r = pl.ds(pl.multiple_of(c * chunk, chunk), chunk)
            kn_s[r, :] = _rms_rows(k_ref[r, :], kn_ref[...]).astype(BF16)
            v_s[r, :] = v_ref[r, :].astype(BF16)
            return carry
        lax.fori_loop(0, seq // chunk, body, 0)

    qn = _rms_rows(q_ref[...], qn_ref[...]).astype(BF16)
    row = lax.broadcasted_iota(jnp.int32, (tq, tq), 0)
    col = lax.broadcasted_iota(jnp.int32, (tq, tq), 1)
    causal = col < row

    def block(j, carry, acc, diagonal):
        r = pl.ds(pl.multiple_of(j * tq, tq), tq)
        z = lax.dot_general(qn, kn_s[r, :], (((1,), (1,)), ((), ())),
                            preferred_element_type=F32) * scale
        lom = _log_sigmoid(-z)
        if diagonal:
            lom = jnp.where(causal, lom, 0.0)
        hi = lom.astype(BF16)
        lo = (lom - hi.astype(F32)).astype(BF16)
        rest = jnp.dot(jnp.concatenate([hi, lo], axis=1), u2_ref[...], preferred_element_type=F32)
        w = jnp.exp(z + lom + rest + carry)
        if diagonal:
            w = jnp.where(causal, w, 0.0)
        acc = acc + jnp.dot(w.astype(BF16), v_s[r, :], preferred_element_type=F32)
        carry = carry + jnp.sum(lom, axis=1, keepdims=True)
        return carry, acc

    carry0 = jnp.zeros((tq, 1), F32)
    acc0 = jnp.zeros((tq, LANES), F32)
    carry, acc = block(qi, carry0, acc0, True)

    def body(it, state):
        return block(qi - 1 - it, state[0], state[1], False)

    carry, acc = lax.fori_loop(0, qi, body, (carry, acc))
    o_ref[...] = acc.astype(o_ref.dtype)


def sb_attention(proj, qn_w, kn_w, *, batch, seq, tq):
    n = proj.shape[0]
    nq = seq // tq
    tri = (jnp.arange(tq)[:, None] > jnp.arange(tq)[None, :]).astype(BF16)
    u2 = jnp.concatenate([tri, tri], axis=0)
    return pl.pallas_call(
        functools.partial(_sb_attn_body, tq=tq, chunk=min(seq, 256)),
        grid=(batch, HEADS, nq),
        in_specs=[
            pl.BlockSpec((tq, LANES), lambda b, h, i: (b * nq + i, h)),
            pl.BlockSpec((seq, LANES), lambda b, h, i: (b, HEADS + h)),
            pl.BlockSpec((seq, LANES), lambda b, h, i: (b, 2 * HEADS + h)),
            pl.BlockSpec((1, LANES), lambda b, h, i: (0, 0)),
            pl.BlockSpec((1, LANES), lambda b, h, i: (0, 0)),
            pl.BlockSpec((2 * tq, tq), lambda b, h, i: (0, 0)),
        ],
        out_specs=pl.BlockSpec((tq, LANES), lambda b, h, i: (b * nq + i, h)),
        out_shape=jax.ShapeDtypeStruct((n, SB_WIDTH), BF16),
        scratch_shapes=[pltpu.VMEM((seq, LANES), BF16), pltpu.VMEM((seq, LANES), BF16)],
        compiler_params=_params(("parallel", "parallel", "arbitrary")),
    )(proj, proj, proj, qn_w.reshape(1, LANES), kn_w.reshape(1, LANES), u2)


def _hgrn_chunk(q, f_logit, val, gate, lbp, nw, l3, state_t):
    c, s = HG_CHUNK, HG_SUB
    log_lb = lbp[0:1, :]
    log_1m_lb = lbp[1:2, :]
    cc = log_1m_lb + _log_sigmoid(f_logit)
    log_f = jnp.maximum(log_lb, cc) + jnp.log(1.0 + jnp.exp(-jnp.abs(log_lb - cc)))
    k = 1.0 - jnp.exp(log_f)

    hi = log_f.astype(BF16)
    r1 = log_f - hi.astype(F32)
    mid = r1.astype(BF16)
    lo = (r1 - mid.astype(F32)).astype(BF16)
    b = jnp.dot(l3, jnp.concatenate([hi, mid, lo], axis=0), preferred_element_type=F32)

    o = lax.dot_general((q * jnp.exp(b)).astype(BF16), state_t.astype(BF16),
                        (((1,), (1,)), ((), ())), preferred_element_type=F32)
    val_bf = val.astype(BF16)

    rows = []
    for blk in range(c // s):
        lo_r = blk * s
        qb, bb = q[lo_r:lo_r + s], b[lo_r:lo_r + s]
        ob = o[lo_r:lo_r + s]
        if blk > 0:
            ref_b = b[lo_r - 1:lo_r]
            qt = (qb * jnp.exp(bb - ref_b)).astype(BF16)
            kt = (k[:lo_r] * jnp.exp(ref_b - b[:lo_r])).astype(BF16)
            sc = lax.dot_general(qt, kt, (((1,), (1,)), ((), ())), preferred_element_type=F32)
            ob = ob + jnp.dot(sc.astype(BF16), val_bf[:lo_r], preferred_element_type=F32)
        trow = lax.broadcasted_iota(jnp.int32, (s, LANES), 0)
        for j in range(s):
            d = jnp.where(trow >= j, bb - bb[j:j + 1], -jnp.inf)
            p = qb * k[lo_r + j:lo_r + j + 1] * jnp.exp(d)
            ob = ob + jnp.sum(p, axis=-1, keepdims=True) * val[lo_r + j:lo_r + j + 1]
        rows.append(ob)
    o = jnp.concatenate(rows, axis=0)

    b_last = b[c - 1:c]
    kd = (k * jnp.exp(b_last - b)).astype(BF16)
    upd = jnp.dot(val.T.astype(BF16), kd, preferred_element_type=F32)
    state_t = state_t * jnp.exp(b_last) + upd

    y = _rms_rows(o, nw) * _sigmoid(gate)
    return y, state_t


def _hgrn_body(q_ref, f_ref, i_ref, g_ref, lbp_ref, nw_ref, l3_ref, o_ref, st_ref):
    @pl.when(pl.program_id(2) == 0)
    def _():
        st_ref[...] = jnp.zeros_like(st_ref)

    n_chunks = q_ref.shape[0] // HG_CHUNK

    def body(ci, state_t):
        r = pl.ds(pl.multiple_of(ci * HG_CHUNK, HG_CHUNK), HG_CHUNK)
        y, state_t = _hgrn_chunk(q_ref[r, :], f_ref[r, :], i_ref[r, :], g_ref[r, :],
                                 lbp_ref[0], nw_ref[...], l3_ref[...], state_t)
        o_ref[r, :] = y.astype(o_ref.dtype)
        return state_t

    st_ref[...] = lax.fori_loop(0, n_chunks, body, st_ref[...])


def hgrn2(proj, lb, nw, *, batch, seq, rows):
    n = proj.shape[0]
    ns = seq // rows
    base = 3 * HEADS
    lbh = lb.reshape(HEADS, 1, LANES).astype(F32)
    lbp = jnp.concatenate([jnp.log(lbh), jnp.log1p(-lbh)], axis=1)
    tri = (jnp.arange(HG_CHUNK)[:, None] >= jnp.arange(HG_CHUNK)[None, :]).astype(BF16)
    l3 = jnp.concatenate([tri, tri, tri], axis=1)

    def col(k):
        return pl.BlockSpec((rows, LANES), lambda b, h, i: (b * ns + i, base + k * HEADS + h))

    return pl.pallas_call(
        _hgrn_body,
        grid=(batch, HEADS, ns),
        in_specs=[
            col(0), col(1), col(2), col(3),
            pl.BlockSpec((1, 2, LANES), lambda b, h, i: (h, 0, 0)),
            pl.BlockSpec((1, LANES), lambda b, h, i: (0, 0)),
            pl.BlockSpec((HG_CHUNK, 3 * HG_CHUNK), lambda b, h, i: (0, 0)),
        ],
        out_specs=pl.BlockSpec((rows, LANES), lambda b, h, i: (b * ns + i, h)),
        out_shape=jax.ShapeDtypeStruct((n, HG_WIDTH), BF16),
        scratch_shapes=[pltpu.VMEM((LANES, LANES), F32)],
        compiler_params=_params(("parallel", "parallel", "arbitrary")),
    )(proj, proj, proj, proj, lbp, nw.reshape(1, LANES), l3)


def _conv_body(a_ref, b_ref, ah_ref, bh_ref, w_ref, cb_ref, lw_ref, lb_ref, o_ref, h_s, c_s, *, rb):
    ts = a_ref.shape[0]
    first = pl.program_id(1) == 0
    halo = ah_ref[...] * _sigmoid(bh_ref[...])
    h_s[0:CONV_HALO, :] = jnp.where(first, 0.0, halo)
    h_s[CONV_HALO:, :] = a_ref[...] * _sigmoid(b_ref[...])
    shift = CONV_HALO - (CONV_TAPS - 1)

    def body(ri, carry):
        r0 = pl.multiple_of(ri * rb, rb)
        for cblk in range(CONV_CH // LANES):
            cs = slice(cblk * LANES, (cblk + 1) * LANES)
            acc = jnp.zeros((rb, LANES), F32) + cb_ref[:, cs]
            window = h_s[pl.ds(r0, rb + CONV_HALO), cs]
            for j in range(CONV_TAPS):
                acc = acc + w_ref[j:j + 1, cs] * window[shift + j:shift + j + rb]
            c_s[pl.ds(r0, rb), cs] = acc
        return carry

    lax.fori_loop(0, ts // rb, body, 0)

    c = c_s[...]
    mu = jnp.mean(c, axis=-1, keepdims=True)
    cen = c - mu
    var = jnp.mean(cen * cen, axis=-1, keepdims=True)
    hn = cen * lax.rsqrt(var + EPS) * lw_ref[...] + lb_ref[...]
    o_ref[...] = (hn * _sigmoid(hn)).astype(o_ref.dtype)


def conformer_conv(proj, conv_w, conv_b, ln_w, ln_b, *, batch, seq, ts):
    n = proj.shape[0]
    ns = seq // ts
    a_blk = (3 * SB_WIDTH + 4 * HG_WIDTH) // CONV_CH
    hpt = ts // CONV_HALO

    def halo_map(cblk):
        return lambda b, i: (jnp.maximum(b * ns * hpt + i * hpt - 1, 0), cblk)

    vec = lambda v: v.reshape(1, CONV_CH)
    return pl.pallas_call(
        functools.partial(_conv_body, rb=32),
        grid=(batch, ns),
        in_specs=[
            pl.BlockSpec((ts, CONV_CH), lambda b, i: (b * ns + i, a_blk)),
            pl.BlockSpec((ts, CONV_CH), lambda b, i: (b * ns + i, a_blk + 1)),
            pl.BlockSpec((CONV_HALO, CONV_CH), halo_map(a_blk)),
            pl.BlockSpec((CONV_HALO, CONV_CH), halo_map(a_blk + 1)),
            pl.BlockSpec((CONV_TAPS, CONV_CH), lambda b, i: (0, 0)),
            pl.BlockSpec((1, CONV_CH), lambda b, i: (0, 0)),
            pl.BlockSpec((1, CONV_CH), lambda b, i: (0, 0)),
            pl.BlockSpec((1, CONV_CH), lambda b, i: (0, 0)),
        ],
        out_specs=pl.BlockSpec((ts, CONV_CH), lambda b, i: (b * ns + i, 0)),
        out_shape=jax.ShapeDtypeStruct((n, CONV_CH), BF16),
        scratch_shapes=[pltpu.VMEM((CONV_HALO + ts, CONV_CH), F32), pltpu.VMEM((ts, CONV_CH), F32)],
        compiler_params=_params(("parallel", "arbitrary")),
    )(proj, proj, proj, proj, conv_w.reshape(CONV_TAPS, CONV_CH), vec(conv_b), vec(ln_w), vec(ln_b))


GATE_BLOCK = 1024


def _merge_body(x_ref, ysb_ref, yhg_ref, yc_ref, g0, g1, g2, g3, g4, g5,
                wsb_ref, whg_ref, wc_ref, wo_ref, o_ref):
    gates = ((g0, g2, g4), (g1, g3, g5))
    acc = x_ref[...]
    for c in range(2):
        cs = slice(c * GATE_BLOCK, (c + 1) * GATE_BLOCK)
        merged = None
        for y_ref, g_ref, w_ref in zip((ysb_ref, yhg_ref, yc_ref), gates[c],
                                       (wsb_ref, whg_ref, wc_ref)):
            term = _sigmoid(g_ref[...]) * jnp.dot(y_ref[...], w_ref[:, cs],
                                                  preferred_element_type=F32)
            merged = term if merged is None else merged + term
        acc = acc + jnp.dot(merged.astype(BF16), wo_ref[cs, :], preferred_element_type=F32)
    o_ref[...] = acc


def merge_branches(x, proj, y_sb, y_hg, y_c, w_sb, w_hg, w_c, w_o, *, tm):
    n, d = x.shape
    assert d == 2 * GATE_BLOCK
    g_blk = (3 * SB_WIDTH + 4 * HG_WIDTH + 2 * CONV_CH) // GATE_BLOCK
    row = lambda width: pl.BlockSpec((tm, width), lambda i: (i, 0))
    gate = lambda k: pl.BlockSpec((tm, GATE_BLOCK), lambda i: (i, g_blk + k))
    whole = lambda a: pl.BlockSpec(a.shape, lambda i: (0, 0), pipeline_mode=pl.Buffered(1))
    return pl.pallas_call(
        _merge_body,
        grid=(n // tm,),
        in_specs=[row(d), row(SB_WIDTH), row(HG_WIDTH), row(CONV_CH)]
                 + [gate(k) for k in range(6)]
                 + [whole(w_sb), whole(w_hg), whole(w_c), whole(w_o)],
        out_specs=row(d),
        out_shape=jax.ShapeDtypeStruct((n, d), F32),
        compiler_params=_params(("parallel",)),
    )(x, y_sb, y_hg, y_c, *([proj] * 6), w_sb, w_hg, w_c, w_o)


def _peer_scores_body(x_ref, nw_ref, wq_ref, k1_ref, k2_ref, hb_ref, s_ref, *, chunk):
    @pl.when(pl.program_id(1) == 0)
    def _():
        def body(c, carry):
            r = pl.ds(pl.multiple_of(c * chunk, chunk), chunk)
            hb_ref[r, :] = _rms_rows(x_ref[r, :], nw_ref[...]).astype(BF16)
            return carry
        lax.fori_loop(0, x_ref.shape[0] // chunk, body, 0)

    q = jnp.dot(hb_ref[...], wq_ref[...], preferred_element_type=F32).astype(BF16)
    nt = (((1,), (1,)), ((), ()))
    s_ref[0] = lax.dot_general(k1_ref[0], q[:, :LANES], nt, preferred_element_type=F32)
    s_ref[1] = lax.dot_general(k2_ref[0], q[:, LANES:], nt, preferred_element_type=F32)


def peer_scores(x, nw, wq, keys1, keys2, *, tm):
    n, d = x.shape
    return pl.pallas_call(
        functools.partial(_peer_scores_body, chunk=min(tm, 256)),
        grid=(n // tm, HEADS),
        in_specs=[
            pl.BlockSpec((tm, d), lambda i, h: (i, 0)),
            pl.BlockSpec((1, d), lambda i, h: (0, 0)),
            pl.BlockSpec((d, 2 * LANES), lambda i, h: (0, h)),
            pl.BlockSpec((1, PEER_NKEYS, LANES), lambda i, h: (h, 0, 0)),
            pl.BlockSpec((1, PEER_NKEYS, LANES), lambda i, h: (h, 0, 0)),
        ],
        out_specs=[
            pl.BlockSpec((tm, d), lambda i, h: (i, 0)),
            pl.BlockSpec((2, PEER_NKEYS, tm), lambda i, h: (h, 0, i)),
        ],
        out_shape=[
            jax.ShapeDtypeStruct((n, d), BF16),
            jax.ShapeDtypeStruct((2 * HEADS, PEER_NKEYS, n), F32),
        ],
        compiler_params=_params(("parallel", "arbitrary")),
    )(x, nw.reshape(1, d), wq, keys1, keys2)


_CANDIDATES = tuple((a, b) for a in range(PEER_TOPK) for b in range(PEER_TOPK)
                    if (a + 1) * (b + 1) <= PEER_TOPK)


def _top16_keys(x):
    nk = x.shape[0]
    kidx = lax.broadcasted_iota(jnp.int32, x.shape, 0)
    vals, idxs = [], []
    for _ in range(PEER_TOPK):
        m = jnp.max(x, axis=0)
        sel = jnp.min(jnp.where(x == m[None], kidx, nk), axis=0)
        x = jnp.where(kidx == sel[None], -jnp.inf, x)
        vals.append(m)
        idxs.append(sel)
    return vals, idxs


def _peer_topk_math(s1, s2):
    v1, i1 = _top16_keys(s1)
    v2, i2 = _top16_keys(s2)
    cand = [v1[a] + v2[b] for a, b in _CANDIDATES]
    expert = [i1[a] * PEER_NKEYS + i2[b] for a, b in _CANDIDATES]
    flat = [a * PEER_TOPK + b for a, b in _CANDIDATES]
    big = PEER_TOPK * PEER_TOPK
    tops, exps = [], []
    for _ in range(PEER_TOPK):
        m = functools.reduce(jnp.maximum, cand)
        sel = functools.reduce(jnp.minimum, [jnp.where(c == m, f, big) for c, f in zip(cand, flat)])
        hit = [sel == f for f in flat]
        exps.append(functools.reduce(jnp.add, [jnp.where(h, e, 0) for h, e in zip(hit, expert)]))
        cand = [jnp.where(h, -jnp.inf, c) for h, c in zip(hit, cand)]
        tops.append(m)
    w = [jnp.exp(t - tops[0]) for t in tops]
    inv = 1.0 / functools.reduce(jnp.add, w)
    return exps, [wk * inv for wk in w]


def _peer_topk_body(s_ref, e_ref, g_ref):
    exps, gates = _peer_topk_math(s_ref[0], s_ref[1])
    for k in range(PEER_TOPK):
        e_ref[0, k] = exps[k]
        g_ref[0, k] = gates[k]


def peer_topk(scores4):
    _, nk, rows, lanes = scores4.shape
    slab = 8
    out_spec = pl.BlockSpec((1, PEER_TOPK, slab, lanes), lambda i, h: (h, 0, i, 0))
    shape = (HEADS, PEER_TOPK, rows, lanes)
    return pl.pallas_call(
        _peer_topk_body,
        grid=(rows // slab, HEADS),
        in_specs=[pl.BlockSpec((2, nk, slab, lanes), lambda i, h: (h, 0, i, 0))],
        out_specs=[out_spec, out_spec],
        out_shape=[jax.ShapeDtypeStruct(shape, jnp.int32), jax.ShapeDtypeStruct(shape, F32)],
        compiler_params=_params(("parallel", "parallel")),
    )(scores4)


W_PITCH = 136


def _peer_gatemat_body(e_ref, g_ref, w_ref, wt_s):
    tb = e_ref.shape[0]
    sub = lax.broadcasted_iota(jnp.int32, (PEER_NKEYS, LANES), 0)

    def build(t, carry):
        e = e_ref[pl.ds(t, 1), :]
        g = g_ref[pl.ds(t, 1), :]
        a_t = jnp.where(sub == (e >> 7), g, 0.0).astype(BF16)
        b_t = jnp.where(sub == (e & 127), 1.0, 0.0).astype(BF16)
        wt = lax.dot_general(a_t, b_t, (((1,), (1,)), ((), ())), preferred_element_type=F32)
        wt_s[pl.ds(pl.multiple_of(t * W_PITCH, 8), PEER_NKEYS), :] = wt
        return carry

    lax.fori_loop(0, tb, build, 0)

    def emit(j, carry):
        rows = wt_s[pl.ds(j, tb, stride=W_PITCH), :]
        w_ref[:, pl.ds(pl.multiple_of(j * LANES, LANES), LANES)] = rows.astype(w_ref.dtype)
        return carry

    lax.fori_loop(0, PEER_NKEYS, emit, 0)


def peer_gate_matrix(experts, gates, *, tb):
    n = experts.shape[0]
    ne = PEER_NKEYS * PEER_NKEYS
    return pl.pallas_call(
        _peer_gatemat_body,
        grid=(n // tb,),
        in_specs=[pl.BlockSpec((tb, LANES), lambda i: (i, 0)),
                  pl.BlockSpec((tb, LANES), lambda i: (i, 0))],
        out_specs=pl.BlockSpec((tb, ne), lambda i: (i, 0)),
        out_shape=jax.ShapeDtypeStruct((n, ne), BF16),
        scratch_shapes=[pltpu.VMEM((tb * W_PITCH, LANES), F32)],
        compiler_params=_params(("parallel",)),
    )(experts, gates)


def _peer_dense_body(x_ref, hb_ref, w_ref, u_ref, v_ref, o_ref):
    @pl.when(pl.program_id(1) == 0)
    def _():
        o_ref[...] = x_ref[...]

    act = lax.dot_general(hb_ref[...], u_ref[...], (((1,), (1,)), ((), ())),
                          preferred_element_type=F32)
    gelu = 0.5 * act * (1.0 + lax.erf(act * (2.0 ** -0.5)))
    p = (w_ref[...].astype(F32) * gelu).astype(BF16)
    o_ref[...] += jnp.dot(p, v_ref[...], preferred_element_type=F32)


def peer_dense(x, hb, wmat, u, v, *, tm, te):
    n, d = x.shape
    ne = u.shape[0]
    return pl.pallas_call(
        _peer_dense_body,
        grid=(n // tm, ne // te),
        in_specs=[
            pl.BlockSpec((tm, d), lambda i, e: (i, 0), pipeline_mode=pl.Buffered(1)),
            pl.BlockSpec((tm, d), lambda i, e: (i, 0)),
            pl.BlockSpec((tm, te), lambda i, e: (i, e)),
            pl.BlockSpec((te, d), lambda i, e: (e, 0)),
            pl.BlockSpec((te, d), lambda i, e: (e, 0)),
        ],
        out_specs=pl.BlockSpec((tm, d), lambda i, e: (i, 0)),
        out_shape=jax.ShapeDtypeStruct((n, d), F32),
        compiler_params=_params(("parallel", "arbitrary")),
    )(x, hb, wmat, u, v)


def _tile(n, want):
    return min(n, want)


def mixer_layer(x2, batch, seq, norm_w, w_in, qn_w, kn_w, lb, hg_nw, conv_w, conv_b, ln_w, ln_b,
                w_sb, w_hg, w_c, w_o):
    n = x2.shape[0]
    proj = norm_matmul(x2, norm_w, w_in.astype(BF16), tm=_tile(n, 1024), tn=1024)
    y_sb = sb_attention(proj, qn_w, kn_w, batch=batch, seq=seq, tq=_tile(seq, 256))
    y_hg = hgrn2(proj, lb, hg_nw, batch=batch, seq=seq, rows=_tile(seq, 512))
    y_c = conformer_conv(proj, conv_w, conv_b, ln_w, ln_b, batch=batch, seq=seq, ts=_tile(seq, 256))
    return merge_branches(x2, proj, y_sb, y_hg, y_c, w_sb.astype(BF16), w_hg.astype(BF16),
                          w_c.astype(BF16), w_o.astype(BF16), tm=_tile(n, 256))


def peer_layer(x2, norm_w, w_query, keys1, keys2, u, v):
    n = x2.shape[0]
    hb, scores = peer_scores(x2, norm_w, w_query.astype(BF16), keys1.astype(BF16),
                             keys2.astype(BF16), tm=_tile(n, 1024))
    scores4 = scores.reshape(2 * HEADS, PEER_NKEYS, n // LANES, LANES)
    experts, gates = peer_topk(scores4)
    to_rows = lambda a: a.reshape(HEADS * PEER_TOPK, n).T
    wmat = peer_gate_matrix(to_rows(experts), to_rows(gates), tb=_tile(n, 128))
    return peer_dense(x2, hb, wmat, u.astype(BF16), v.astype(BF16), tm=_tile(n, 1024), te=512)


def _lower_bounds(p):
    c = jnp.cumsum(jax.nn.softmax(p.astype(F32), axis=0), axis=0)
    return c - c[0:1]


def kernel(x, attn_norm_w, w_in, sb_q_norm_w, sb_k_norm_w, hg_lower_bounds, hg_norm_w, conv_w, conv_b, conv_ln_w, conv_ln_b, w_br_sb, w_br_hg, w_br_conv, w_o, ffn_norm_w, peer_w_query, peer_keys1, peer_keys2, peer_u, peer_v):
    batch, seq, d = x.shape
    lbs = _lower_bounds(hg_lower_bounds)
    x2 = x.reshape(batch * seq, d)
    for l in range(w_in.shape[0]):
        x2 = mixer_layer(x2, batch, seq, attn_norm_w[l], w_in[l], sb_q_norm_w[l], sb_k_norm_w[l],
                         lbs[l], hg_norm_w[l], conv_w[l], conv_b[l], conv_ln_w[l], conv_ln_b[l],
                         w_br_sb[l], w_br_hg[l], w_br_conv[l], w_o[l])
        x2 = peer_layer(x2, ffn_norm_w[l], peer_w_query[l], peer_keys1[l], peer_keys2[l],
                        peer_u[l], peer_v[l])
    return x2.reshape(batch, seq, d)
```

```python
import functools

import jax
import jax.numpy as jnp
from jax import lax
from jax.experimental import pallas as pl
from jax.experimental.pallas import tpu as pltpu

F32 = jnp.float32
BF16 = jnp.bfloat16
EPS = 1e-6

LANES = 128
HEADS = 8
SB_WIDTH = HEADS * LANES
HG_WIDTH = HEADS * LANES
CONV_CH = 1024
CONV_TAPS = 31
CONV_HALO = 32
HG_CHUNK = 64
HG_SUB = 16
PEER_TOPK = 16
PEER_NKEYS = 128
VMEM_LIMIT = 56 * 1024 * 1024


def _params(sem):
    return pltpu.CompilerParams(dimension_semantics=sem, vmem_limit_bytes=VMEM_LIMIT)


def _rms_rows(x, w):
    ms = jnp.mean(x * x, axis=-1, keepdims=True)
    return x * lax.rsqrt(ms + EPS) * w


def _log_sigmoid(z):
    return jnp.minimum(z, 0.0) - jnp.log(1.0 + jnp.exp(-jnp.abs(z)))


def _sigmoid(z):
    return 1.0 / (1.0 + jnp.exp(-z))


def _norm_matmul_body(x_ref, nw_ref, w_ref, o_ref, h_ref, *, chunk):
    @pl.when(pl.program_id(1) == 0)
    def _():
        def body(c, carry):
            r = pl.ds(pl.multiple_of(c * chunk, chunk), chunk)
            h_ref[r, :] = _rms_rows(x_ref[r, :], nw_ref[...]).astype(BF16)
            return carry
        lax.fori_loop(0, x_ref.shape[0] // chunk, body, 0)

    o_ref[...] = jnp.dot(h_ref[...], w_ref[...], preferred_element_type=F32).astype(o_ref.dtype)


def norm_matmul(x, nw, w, *, tm, tn, out_dtype=F32):
    n, d = x.shape
    m = w.shape[1]
    return pl.pallas_call(
        functools.partial(_norm_matmul_body, chunk=min(tm, 256)),
        grid=(n // tm, m // tn),
        in_specs=[
            pl.BlockSpec((tm, d), lambda i, j: (i, 0)),
            pl.BlockSpec((1, d), lambda i, j: (0, 0)),
            pl.BlockSpec((d, tn), lambda i, j: (0, j)),
        ],
        out_specs=pl.BlockSpec((tm, tn), lambda i, j: (i, j)),
        out_shape=jax.ShapeDtypeStruct((n, m), out_dtype),
        scratch_shapes=[pltpu.VMEM((tm, d), BF16)],
        compiler_params=_params(("parallel", "arbitrary")),
    )(x, nw.reshape(1, d), w)


def _sb_attn_body(q_ref, k_ref, v_ref, qn_ref, kn_ref, u2_ref, o_ref, kn_s, v_s, *, tq, chunk):
    qi = pl.program_id(2)
    seq = k_ref.shape[0]
    scale = LANES ** -0.5

    @pl.when(qi == 0)
    def _():
        def body(c, carry):
            r = pl.ds(pl.multiple_of(c * chunk, chunk), chunk)
            kn_s[r, :] = _rms_rows(k_ref[r, :], kn_ref[...]).astype(BF16)
            v_s[r, :] = v_ref[r, :].astype(BF16)
            return carry
        lax.fori_loop(0, seq // chunk, body, 0)

    qn = _rms_rows(q_ref[...], qn_ref[...]).astype(BF16)
    row = lax.broadcasted_iota(jnp.int32, (tq, tq), 0)
    col = lax.broadcasted_iota(jnp.int32, (tq, tq), 1)
    causal = col < row

    def rows_block(qn_h, kb, vb, carry, acc, mask):
        z = lax.dot_general(qn_h, kb, (((1,), (1,)), ((), ())), preferred_element_type=F32) * scale
        lom = _log_sigmoid(-z)
        if mask is not None:
            lom = jnp.where(mask, lom, 0.0)
        hi = lom.astype(BF16)
        lo = (lom - hi.astype(F32)).astype(BF16)
        rest = jnp.dot(jnp.concatenate([hi, lo], axis=1), u2_ref[...], preferred_element_type=F32)
        w = jnp.exp(z + lom + rest + carry)
        if mask is not None:
            w = jnp.where(mask, w, 0.0)
        acc = acc + jnp.dot(w.astype(BF16), vb, preferred_element_type=F32)
        carry = carry + jnp.sum(lom, axis=1, keepdims=True)
        return carry, acc

    def block(j, carry, acc, diagonal):
        r = pl.ds(pl.multiple_of(j * tq, tq), tq)
        return rows_block(qn, kn_s[r, :], v_s[r, :], carry, acc, causal if diagonal else None)

    carry0 = jnp.zeros((tq, 1), F32)
    acc0 = jnp.zeros((tq, LANES), F32)
    carry, acc = block(qi, carry0, acc0, True)

    def body(it, state):
        j = qi - 1 - 2 * it
        carry, acc = block(j, state[0], state[1], False)
        return block(j - 1, carry, acc, False)

    carry, acc = lax.fori_loop(0, qi // 2, body, (carry, acc))
    carry, acc = lax.cond(qi % 2 == 1,
                          lambda c, a: block(0, c, a, False),
                          lambda c, a: (c, a), carry, acc)
    o_ref[...] = acc.astype(o_ref.dtype)


def sb_attention(proj, qn_w, kn_w, *, batch, seq, tq):
    n = proj.shape[0]
    nq = seq // tq
    tri = (jnp.arange(tq)[:, None] > jnp.arange(tq)[None, :]).astype(BF16)
    u2 = jnp.concatenate([tri, tri], axis=0)
    return pl.pallas_call(
        functools.partial(_sb_attn_body, tq=tq, chunk=min(seq, 256)),
        grid=(batch, HEADS, nq),
        in_specs=[
            pl.BlockSpec((tq, LANES), lambda b, h, i: (b * nq + i, h)),
            pl.BlockSpec((seq, LANES), lambda b, h, i: (b, HEADS + h)),
            pl.BlockSpec((seq, LANES), lambda b, h, i: (b, 2 * HEADS + h)),
            pl.BlockSpec((1, LANES), lambda b, h, i: (0, 0)),
            pl.BlockSpec((1, LANES), lambda b, h, i: (0, 0)),
            pl.BlockSpec((2 * tq, tq), lambda b, h, i: (0, 0)),
        ],
        out_specs=pl.BlockSpec((tq, LANES), lambda b, h, i: (b * nq + i, h)),
        out_shape=jax.ShapeDtypeStruct((n, SB_WIDTH), BF16),
        scratch_shapes=[pltpu.VMEM((seq, LANES), BF16), pltpu.VMEM((seq, LANES), BF16)],
        compiler_params=_params(("parallel", "parallel", "arbitrary")),
    )(proj, proj, proj, qn_w.reshape(1, LANES), kn_w.reshape(1, LANES), u2)


def _hgrn_chunk(q, f_logit, val, gate, lbp, nw, l3, state_t):
    c, s = HG_CHUNK, HG_SUB
    log_lb = lbp[0:1, :]
    log_1m_lb = lbp[1:2, :]
    cc = log_1m_lb + _log_sigmoid(f_logit)
    log_f = jnp.maximum(log_lb, cc) + jnp.log(1.0 + jnp.exp(-jnp.abs(log_lb - cc)))
    k = 1.0 - jnp.exp(log_f)

    hi = log_f.astype(BF16)
    r1 = log_f - hi.astype(F32)
    mid = r1.astype(BF16)
    lo = (r1 - mid.astype(F32)).astype(BF16)
    b = jnp.dot(l3, jnp.concatenate([hi, mid, lo], axis=0), preferred_element_type=F32)

    o = lax.dot_general((q * jnp.exp(b)).astype(BF16), state_t.astype(BF16),
                        (((1,), (1,)), ((), ())), preferred_element_type=F32)
    val_bf = val.astype(BF16)

    rows = []
    for blk in range(c // s):
        lo_r = blk * s
        qb, bb = q[lo_r:lo_r + s], b[lo_r:lo_r + s]
        ob = o[lo_r:lo_r + s]
        if blk > 0:
            ref_b = b[lo_r - 1:lo_r]
            qt = (qb * jnp.exp(bb - ref_b)).astype(BF16)
            kt = (k[:lo_r] * jnp.exp(ref_b - b[:lo_r])).astype(BF16)
            sc = lax.dot_general(qt, kt, (((1,), (1,)), ((), ())), preferred_element_type=F32)
            ob = ob + jnp.dot(sc.astype(BF16), val_bf[:lo_r], preferred_element_type=F32)
        trow = lax.broadcasted_iota(jnp.int32, (s, LANES), 0)
        for j in range(s):
            d = jnp.where(trow >= j, bb - bb[j:j + 1], -jnp.inf)
            p = qb * k[lo_r + j:lo_r + j + 1] * jnp.exp(d)
            ob = ob + jnp.sum(p, axis=-1, keepdims=True) * val[lo_r + j:lo_r + j + 1]
        rows.append(ob)
    o = jnp.concatenate(rows, axis=0)

    b_last = b[c - 1:c]
    kd = (k * jnp.exp(b_last - b)).astype(BF16)
    upd = jnp.dot(val.T.astype(BF16), kd, preferred_element_type=F32)
    state_t = state_t * jnp.exp(b_last) + upd

    y = _rms_rows(o, nw) * _sigmoid(gate)
    return y, state_t


def _hgrn_body(q_ref, f_ref, i_ref, g_ref, lbp_ref, nw_ref, l3_ref, o_ref, st_ref):
    @pl.when(pl.program_id(2) == 0)
    def _():
        st_ref[...] = jnp.zeros_like(st_ref)

    n_chunks = q_ref.shape[0] // HG_CHUNK

    def body(ci, state_t):
        r = pl.ds(pl.multiple_of(ci * HG_CHUNK, HG_CHUNK), HG_CHUNK)
        y, state_t = _hgrn_chunk(q_ref[r, :], f_ref[r, :], i_ref[r, :], g_ref[r, :],
                                 lbp_ref[0], nw_ref[...], l3_ref[...], state_t)
        o_ref[r, :] = y.astype(o_ref.dtype)
        return state_t

    st_ref[...] = lax.fori_loop(0, n_chunks, body, st_ref[...], unroll=8)


def hgrn2(proj, lb, nw, *, batch, seq, rows):
    n = proj.shape[0]
    ns = seq // rows
    base = 3 * HEADS
    lbh = lb.reshape(HEADS, 1, LANES).astype(F32)
    lbp = jnp.concatenate([jnp.log(lbh), jnp.log1p(-lbh)], axis=1)
    tri = (jnp.arange(HG_CHUNK)[:, None] >= jnp.arange(HG_CHUNK)[None, :]).astype(BF16)
    l3 = jnp.concatenate([tri, tri, tri], axis=1)

    def col(k):
        return pl.BlockSpec((rows, LANES), lambda b, h, i: (b * ns + i, base + k * HEADS + h))

    return pl.pallas_call(
        _hgrn_body,
        grid=(batch, HEADS, ns),
        in_specs=[
            col(0), col(1), col(2), col(3),
            pl.BlockSpec((1, 2, LANES), lambda b, h, i: (h, 0, 0)),
            pl.BlockSpec((1, LANES), lambda b, h, i: (0, 0)),
            pl.BlockSpec((HG_CHUNK, 3 * HG_CHUNK), lambda b, h, i: (0, 0)),
        ],
        out_specs=pl.BlockSpec((rows, LANES), lambda b, h, i: (b * ns + i, h)),
        out_shape=jax.ShapeDtypeStruct((n, HG_WIDTH), BF16),
        scratch_shapes=[pltpu.VMEM((LANES, LANES), F32)],
        compiler_params=_params(("parallel", "parallel", "arbitrary")),
    )(proj, proj, proj, proj, lbp, nw.reshape(1, LANES), l3)


def _conv_body(a_ref, b_ref, ah_ref, bh_ref, w_ref, cb_ref, lw_ref, lb_ref, o_ref, h_s, c_s, *, rb):
    ts = a_ref.shape[0]
    first = pl.program_id(1) == 0
    halo = ah_ref[...] * _sigmoid(bh_ref[...])
    h_s[0:CONV_HALO, :] = jnp.where(first, 0.0, halo)
    h_s[CONV_HALO:, :] = a_ref[...] * _sigmoid(b_ref[...])
    shift = CONV_HALO - (CONV_TAPS - 1)

    def body(ri, carry):
        r0 = pl.multiple_of(ri * rb, rb)
        for cblk in range(CONV_CH // LANES):
            cs = slice(cblk * LANES, (cblk + 1) * LANES)
            acc = jnp.zeros((rb, LANES), F32) + cb_ref[:, cs]
            window = h_s[pl.ds(r0, rb + CONV_HALO), cs]
            for j in range(CONV_TAPS):
                acc = acc + w_ref[j:j + 1, cs] * window[shift + j:shift + j + rb]
            c_s[pl.ds(r0, rb), cs] = acc
        return carry

    lax.fori_loop(0, ts // rb, body, 0)

    c = c_s[...]
    mu = jnp.mean(c, axis=-1, keepdims=True)
    cen = c - mu
    var = jnp.mean(cen * cen, axis=-1, keepdims=True)
    hn = cen * lax.rsqrt(var + EPS) * lw_ref[...] + lb_ref[...]
    o_ref[...] = (hn * _sigmoid(hn)).astype(o_ref.dtype)


def conformer_conv(proj, conv_w, conv_b, ln_w, ln_b, *, batch, seq, ts):
    n = proj.shape[0]
    ns = seq // ts
    a_blk = (3 * SB_WIDTH + 4 * HG_WIDTH) // CONV_CH
    hpt = ts // CONV_HALO

    def halo_map(cblk):
        return lambda b, i: (jnp.maximum(b * ns * hpt + i * hpt - 1, 0), cblk)

    vec = lambda v: v.reshape(1, CONV_CH)
    return pl.pallas_call(
        functools.partial(_conv_body, rb=32),
        grid=(batch, ns),
        in_specs=[
            pl.BlockSpec((ts, CONV_CH), lambda b, i: (b * ns + i, a_blk)),
            pl.BlockSpec((ts, CONV_CH), lambda b, i: (b * ns + i, a_blk + 1)),
            pl.BlockSpec((CONV_HALO, CONV_CH), halo_map(a_blk)),
            pl.BlockSpec((CONV_HALO, CONV_CH), halo_map(a_blk + 1)),
            pl.BlockSpec((CONV_TAPS, CONV_CH), lambda b, i: (0, 0)),
            pl.BlockSpec((1, CONV_CH), lambda b, i: (0, 0)),
            pl.BlockSpec((1, CONV_CH), lambda b, i: (0, 0)),
            pl.BlockSpec((1, CONV_CH), lambda b, i: (0, 0)),
        ],
        out_specs=pl.BlockSpec((ts, CONV_CH), lambda b, i: (b * ns + i, 0)),
        out_shape=jax.ShapeDtypeStruct((n, CONV_CH), BF16),
        scratch_shapes=[pltpu.VMEM((CONV_HALO + ts, CONV_CH), F32), pltpu.VMEM((ts, CONV_CH), F32)],
        compiler_params=_params(("parallel", "arbitrary")),
    )(proj, proj, proj, proj, conv_w.reshape(CONV_TAPS, CONV_CH), vec(conv_b), vec(ln_w), vec(ln_b))


GATE_BLOCK = 1024


def _merge_body(x_ref, ysb_ref, yhg_ref, yc_ref, g0, g1, g2, g3, g4, g5,
                wsb_ref, whg_ref, wc_ref, wo_ref, o_ref):
    gates = ((g0, g2, g4), (g1, g3, g5))
    acc = x_ref[...]
    for c in range(2):
        cs = slice(c * GATE_BLOCK, (c + 1) * GATE_BLOCK)
        merged = None
        for y_ref, g_ref, w_ref in zip((ysb_ref, yhg_ref, yc_ref), gates[c],
                                       (wsb_ref, whg_ref, wc_ref)):
            term = _sigmoid(g_ref[...]) * jnp.dot(y_ref[...], w_ref[:, cs],
                                                  preferred_element_type=F32)
            merged = term if merged is None else merged + term
        acc = acc + jnp.dot(merged.astype(BF16), wo_ref[cs, :], preferred_element_type=F32)
    o_ref[...] = acc


def merge_branches(x, proj, y_sb, y_hg, y_c, w_sb, w_hg, w_c, w_o, *, tm):
    n, d = x.shape
    assert d == 2 * GATE_BLOCK
    g_blk = (3 * SB_WIDTH + 4 * HG_WIDTH + 2 * CONV_CH) // GATE_BLOCK
    row = lambda width: pl.BlockSpec((tm, width), lambda i: (i, 0))
    gate = lambda k: pl.BlockSpec((tm, GATE_BLOCK), lambda i: (i, g_blk + k))
    whole = lambda a: pl.BlockSpec(a.shape, lambda i: (0, 0), pipeline_mode=pl.Buffered(1))
    return pl.pallas_call(
        _merge_body,
        grid=(n // tm,),
        in_specs=[row(d), row(SB_WIDTH), row(HG_WIDTH), row(CONV_CH)]
                 + [gate(k) for k in range(6)]
                 + [whole(w_sb), whole(w_hg), whole(w_c), whole(w_o)],
        out_specs=row(d),
        out_shape=jax.ShapeDtypeStruct((n, d), F32),
        compiler_params=_params(("parallel",)),
    )(x, y_sb, y_hg, y_c, *([proj] * 6), w_sb, w_hg, w_c, w_o)


def _peer_scores_body(x_ref, nw_ref, wq_ref, k1_ref, k2_ref, hb_ref, s_ref, *, chunk):
    @pl.when(pl.program_id(1) == 0)
    def _():
        def body(c, carry):
            r = pl.ds(pl.multiple_of(c * chunk, chunk), chunk)
            hb_ref[r, :] = _rms_rows(x_ref[r, :], nw_ref[...]).astype(BF16)
            return carry
        lax.fori_loop(0, x_ref.shape[0] // chunk, body, 0)

    q = jnp.dot(hb_ref[...], wq_ref[...], preferred_element_type=F32).astype(BF16)
    nt = (((1,), (1,)), ((), ()))
    s_ref[0] = lax.dot_general(k1_ref[0], q[:, :LANES], nt, preferred_element_type=F32)
    s_ref[1] = lax.dot_general(k2_ref[0], q[:, LANES:], nt, preferred_element_type=F32)


def peer_scores(x, nw, wq, keys1, keys2, *, tm):
    n, d = x.shape
    return pl.pallas_call(
        functools.partial(_peer_scores_body, chunk=min(tm, 256)),
        grid=(n // tm, HEADS),
        in_specs=[
            pl.BlockSpec((tm, d), lambda i, h: (i, 0)),
            pl.BlockSpec((1, d), lambda i, h: (0, 0)),
            pl.BlockSpec((d, 2 * LANES), lambda i, h: (0, h)),
            pl.BlockSpec((1, PEER_NKEYS, LANES), lambda i, h: (h, 0, 0)),
            pl.BlockSpec((1, PEER_NKEYS, LANES), lambda i, h: (h, 0, 0)),
        ],
        out_specs=[
            pl.BlockSpec((tm, d), lambda i, h: (i, 0)),
            pl.BlockSpec((2, PEER_NKEYS, tm), lambda i, h: (h, 0, i)),
        ],
        out_shape=[
            jax.ShapeDtypeStruct((n, d), BF16),
            jax.ShapeDtypeStruct((2 * HEADS, PEER_NKEYS, n), F32),
        ],
        compiler_params=_params(("parallel", "arbitrary")),
    )(x, nw.reshape(1, d), wq, keys1, keys2)


_CANDIDATES = tuple((a, b) for a in range(PEER_TOPK) for b in range(PEER_TOPK)
                    if (a + 1) * (b + 1) <= PEER_TOPK)


def _top16_keys(x):
    nk = x.shape[0]
    kidx = lax.broadcasted_iota(jnp.int32, x.shape, 0)
    vals, idxs = [], []
    for _ in range(PEER_TOPK):
        m = jnp.max(x, axis=0)
        sel = jnp.min(jnp.where(x == m[None], kidx, nk), axis=0)
        x = jnp.where(kidx == sel[None], -jnp.inf, x)
        vals.append(m)
        idxs.append(sel)
    return vals, idxs


def _peer_topk_math(s1, s2):
    v1, i1 = _top16_keys(s1)
    v2, i2 = _top16_keys(s2)
    cand = [v1[a] + v2[b] for a, b in _CANDIDATES]
    expert = [i1[a] * PEER_NKEYS + i2[b] for a, b in _CANDIDATES]
    flat = [a * PEER_TOPK + b for a, b in _CANDIDATES]
    big = PEER_TOPK * PEER_TOPK
    tops, exps = [], []
    for _ in range(PEER_TOPK):
        m = functools.reduce(jnp.maximum, cand)
        sel = functools.reduce(jnp.minimum, [jnp.where(c == m, f, big) for c, f in zip(cand, flat)])
        hit = [sel == f for f in flat]
        exps.append(functools.reduce(jnp.add, [jnp.where(h, e, 0) for h, e in zip(hit, expert)]))
        cand = [jnp.where(h, -jnp.inf, c) for h, c in zip(hit, cand)]
        tops.append(m)
    w = [jnp.exp(t - tops[0]) for t in tops]
    inv = 1.0 / functools.reduce(jnp.add, w)
    return exps, [wk * inv for wk in w]


def _peer_topk_body(s_ref, e_ref, g_ref):
    exps, gates = _peer_topk_math(s_ref[0], s_ref[1])
    for k in range(PEER_TOPK):
        e_ref[0, k] = exps[k]
        g_ref[0, k] = gates[k]


def peer_topk(scores4):
    _, nk, rows, lanes = scores4.shape
    slab = 8
    out_spec = pl.BlockSpec((1, PEER_TOPK, slab, lanes), lambda i, h: (h, 0, i, 0))
    shape = (HEADS, PEER_TOPK, rows, lanes)
    return pl.pallas_call(
        _peer_topk_body,
        grid=(rows // slab, HEADS),
        in_specs=[pl.BlockSpec((2, nk, slab, lanes), lambda i, h: (h, 0, i, 0))],
        out_specs=[out_spec, out_spec],
        out_shape=[jax.ShapeDtypeStruct(shape, jnp.int32), jax.ShapeDtypeStruct(shape, F32)],
        compiler_params=_params(("parallel", "parallel")),
    )(scores4)


W_PITCH = 136


def _peer_gatemat_body(e_ref, g_ref, w_ref, wt_s):
    tb = e_ref.shape[0]
    sub = lax.broadcasted_iota(jnp.int32, (PEER_NKEYS, LANES), 0)

    def build(t, carry):
        e = e_ref[pl.ds(t, 1), :]
        g = g_ref[pl.ds(t, 1), :]
        a_t = jnp.where(sub == (e >> 7), g, 0.0).astype(BF16)
        b_t = jnp.where(sub == (e & 127), 1.0, 0.0).astype(BF16)
        wt = lax.dot_general(a_t, b_t, (((1,), (1,)), ((), ())), preferred_element_type=F32)
        wt_s[pl.ds(pl.multiple_of(t * W_PITCH, 8), PEER_NKEYS), :] = wt
        return carry

    lax.fori_loop(0, tb, build, 0, unroll=8)

    def emit(j, carry):
        rows = wt_s[pl.ds(j, tb, stride=W_PITCH), :]
        w_ref[:, pl.ds(pl.multiple_of(j * LANES, LANES), LANES)] = rows.astype(w_ref.dtype)
        return carry

    lax.fori_loop(0, PEER_NKEYS, emit, 0, unroll=4)


def peer_gate_matrix(experts, gates, *, tb):
    n = experts.shape[0]
    ne = PEER_NKEYS * PEER_NKEYS
    return pl.pallas_call(
        _peer_gatemat_body,
        grid=(n // tb,),
        in_specs=[pl.BlockSpec((tb, LANES), lambda i: (i, 0)),
                  pl.BlockSpec((tb, LANES), lambda i: (i, 0))],
        out_specs=pl.BlockSpec((tb, ne), lambda i: (i, 0)),
        out_shape=jax.ShapeDtypeStruct((n, ne), BF16),
        scratch_shapes=[pltpu.VMEM((tb * W_PITCH, LANES), F32)],
        compiler_params=_params(("parallel",)),
    )(experts, gates)


def _peer_dense_body(x_ref, hb_ref, w_ref, u_ref, v_ref, o_ref):
    @pl.when(pl.program_id(1) == 0)
    def _():
        o_ref[...] = x_ref[...]

    act = lax.dot_general(hb_ref[...], u_ref[...], (((1,), (1,)), ((), ())),
                          preferred_element_type=F32)
    gelu = 0.5 * act * (1.0 + lax.erf(act * (2.0 ** -0.5)))
    p = (w_ref[...].astype(F32) * gelu).astype(BF16)
    o_ref[...] += jnp.dot(p, v_ref[...], preferred_element_type=F32)


def peer_dense(x, hb, wmat, u, v, *, tm, te):
    n, d = x.shape
    ne = u.shape[0]
    return pl.pallas_call(
        _peer_dense_body,
        grid=(n // tm, ne // te),
        in_specs=[
            pl.BlockSpec((tm, d), lambda i, e: (i, 0), pipeline_mode=pl.Buffered(1)),
            pl.BlockSpec((tm, d), lambda i, e: (i, 0)),
            pl.BlockSpec((tm, te), lambda i, e: (i, e)),
            pl.BlockSpec((te, d), lambda i, e: (e, 0)),
            pl.BlockSpec((te, d), lambda i, e: (e, 0)),
        ],
        out_specs=pl.BlockSpec((tm, d), lambda i, e: (i, 0)),
        out_shape=jax.ShapeDtypeStruct((n, d), F32),
        compiler_params=_params(("parallel", "arbitrary")),
    )(x, hb, wmat, u, v)


def _tile(n, want):
    return min(n, want)


def mixer_layer(x2, batch, seq, norm_w, w_in, qn_w, kn_w, lb, hg_nw, conv_w, conv_b, ln_w, ln_b,
                w_sb, w_hg, w_c, w_o):
    n = x2.shape[0]
    proj = norm_matmul(x2, norm_w, w_in.astype(BF16), tm=_tile(n, 1024), tn=1024)
    y_sb = sb_attention(proj, qn_w, kn_w, batch=batch, seq=seq, tq=_tile(seq, 256))
    y_hg = hgrn2(proj, lb, hg_nw, batch=batch, seq=seq, rows=_tile(seq, 512))
    y_c = conformer_conv(proj, conv_w, conv_b, ln_w, ln_b, batch=batch, seq=seq, ts=_tile(seq, 256))
    return merge_branches(x2, proj, y_sb, y_hg, y_c, w_sb.astype(BF16), w_hg.astype(BF16),
                          w_c.astype(BF16), w_o.astype(BF16), tm=_tile(n, 256))


def peer_layer(x2, norm_w, w_query, keys1, keys2, u, v):
    n = x2.shape[0]
    hb, scores = peer_scores(x2, norm_w, w_query.astype(BF16), keys1.astype(BF16),
                             keys2.astype(BF16), tm=_tile(n, 1024))
    scores4 = scores.reshape(2 * HEADS, PEER_NKEYS, n // LANES, LANES)
    experts, gates = peer_topk(scores4)
    to_rows = lambda a: a.reshape(HEADS * PEER_TOPK, n).T
    wmat = peer_gate_matrix(to_rows(experts), to_rows(gates), tb=_tile(n, 128))
    return peer_dense(x2, hb, wmat, u.astype(BF16), v.astype(BF16), tm=_tile(n, 1024), te=512)


def _lower_bounds(p):
    c = jnp.cumsum(jax.nn.softmax(p.astype(F32), axis=0), axis=0)
    return c - c[0:1]


def kernel(x, attn_norm_w, w_in, sb_q_norm_w, sb_k_norm_w, hg_lower_bounds, hg_norm_w, conv_w, conv_b, conv_ln_w, conv_ln_b, w_br_sb, w_br_hg, w_br_conv, w_o, ffn_norm_w, peer_w_query, peer_keys1, peer_keys2, peer_u, peer_v):
    batch, seq, d = x.shape
    lbs = _lower_bounds(hg_lower_bounds)
    x2 = x.reshape(batch * seq, d)
    for l in range(w_in.shape[0]):
        x2 = mixer_layer(x2, batch, seq, attn_norm_w[l], w_in[l], sb_q_norm_w[l], sb_k_norm_w[l],
                         lbs[l], hg_norm_w[l], conv_w[l], conv_b[l], conv_ln_w[l], conv_ln_b[l],
                         w_br_sb[l], w_br_hg[l], w_br_conv[l], w_o[l])
        x2 = peer_layer(x2, ffn_norm_w[l], peer_w_query[l], peer_keys1[l], peer_keys2[l],
                        peer_u[l], peer_v[l])
    return x2.reshape(batch, seq, d)
```

```python
import functools

import jax
import jax.numpy as jnp
from jax import lax
from jax.experimental import pallas as pl
from jax.experimental.pallas import tpu as pltpu

F32 = jnp.float32
BF16 = jnp.bfloat16
EPS = 1e-6

LANES = 128
HEADS = 8
SB_WIDTH = HEADS * LANES
HG_WIDTH = HEADS * LANES
CONV_CH = 1024
CONV_TAPS = 31
CONV_HALO = 32
HG_CHUNK = 64
HG_SUB = 16
PEER_TOPK = 16
PEER_NKEYS = 128
VMEM_LIMIT = 56 * 1024 * 1024


def _params(sem):
    return pltpu.CompilerParams(dimension_semantics=sem, vmem_limit_bytes=VMEM_LIMIT)


def _rms_rows(x, w):
    ms = jnp.mean(x * x, axis=-1, keepdims=True)
    return x * lax.rsqrt(ms + EPS) * w


def _log_sigmoid(z):
    return jnp.minimum(z, 0.0) - jnp.log(1.0 + jnp.exp(-jnp.abs(z)))


def _sigmoid(z):
    return 1.0 / (1.0 + jnp.exp(-z))


def _norm_matmul_body(x_ref, nw_ref, w_ref, o_ref, h_ref, *, chunk):
    @pl.when(pl.program_id(1) == 0)
    def _():
        def body(c, carry):
            r = pl.ds(pl.multiple_of(c * chunk, chunk), chunk)
            h_ref[r, :] = _rms_rows(x_ref[r, :], nw_ref[...]).astype(BF16)
            return carry
        lax.fori_loop(0, x_ref.shape[0] // chunk, body, 0)

    o_ref[...] = jnp.dot(h_ref[...], w_ref[...], preferred_element_type=F32).astype(o_ref.dtype)


def norm_matmul(x, nw, w, *, tm, tn, out_dtype=F32):
    n, d = x.shape
    m = w.shape[1]
    return pl.pallas_call(
        functools.partial(_norm_matmul_body, chunk=min(tm, 256)),
        grid=(n // tm, m // tn),
        in_specs=[
            pl.BlockSpec((tm, d), lambda i, j: (i, 0)),
            pl.BlockSpec((1, d), lambda i, j: (0, 0)),
            pl.BlockSpec((d, tn), lambda i, j: (0, j)),
        ],
        out_specs=pl.BlockSpec((tm, tn), lambda i, j: (i, j)),
        out_shape=jax.ShapeDtypeStruct((n, m), out_dtype),
        scratch_shapes=[pltpu.VMEM((tm, d), BF16)],
        compiler_params=_params(("parallel", "arbitrary")),
    )(x, nw.reshape(1, d), w)


def _sb_attn_body(q_ref, k_ref, v_ref, qn_ref, kn_ref, u2_ref, u2w_ref, o_ref, kn_s, v_s, *,
                  tq, chunk):
    qi = pl.program_id(2)
    seq = k_ref.shape[0]
    scale = LANES ** -0.5

    @pl.when(qi == 0)
    def _():
        def body(c, carry):
            r = pl.ds(pl.multiple_of(c * chunk, chunk), chunk)
            kn_s[r, :] = _rms_rows(k_ref[r, :], kn_ref[...]).astype(BF16)
            v_s[r, :] = v_ref[r, :].astype(BF16)
            return carry
        lax.fori_loop(0, seq // chunk, body, 0)

    qn = _rms_rows(q_ref[...], qn_ref[...]).astype(BF16)
    row = lax.broadcasted_iota(jnp.int32, (tq, tq), 0)
    col = lax.broadcasted_iota(jnp.int32, (tq, tq), 1)
    causal = col < row

    def keys_block(start, width, tri_ref, carry, acc, mask):
        r = pl.ds(pl.multiple_of(start, tq), width)
        z = lax.dot_general(qn, kn_s[r, :], (((1,), (1,)), ((), ())),
                            preferred_element_type=F32) * scale
        lom = _log_sigmoid(-z)
        if mask is not None:
            lom = jnp.where(mask, lom, 0.0)
        hi = lom.astype(BF16)
        lo = (lom - hi.astype(F32)).astype(BF16)
        rest = jnp.dot(jnp.concatenate([hi, lo], axis=1), tri_ref[...], preferred_element_type=F32)
        w = jnp.exp(z + lom + rest + carry)
        if mask is not None:
            w = jnp.where(mask, w, 0.0)
        acc = acc + jnp.dot(w.astype(BF16), v_s[r, :], preferred_element_type=F32)
        carry = carry + jnp.sum(lom, axis=1, keepdims=True)
        return carry, acc

    carry0 = jnp.zeros((tq, 1), F32)
    acc0 = jnp.zeros((tq, LANES), F32)
    carry, acc = keys_block(qi * tq, tq, u2_ref, carry0, acc0, causal)

    def body(it, state):
        return keys_block((qi - 2 - 2 * it) * tq, 2 * tq, u2w_ref, state[0], state[1], None)

    carry, acc = lax.fori_loop(0, qi // 2, body, (carry, acc))
    carry, acc = lax.cond(qi % 2 == 1,
                          lambda c, a: keys_block(0, tq, u2_ref, c, a, None),
                          lambda c, a: (c, a), carry, acc)
    o_ref[...] = acc.astype(o_ref.dtype)


def sb_attention(proj, qn_w, kn_w, *, batch, seq, tq):
    n = proj.shape[0]
    nq = seq // tq
    def stacked_tri(width):
        tri = (jnp.arange(width)[:, None] > jnp.arange(width)[None, :]).astype(BF16)
        return jnp.concatenate([tri, tri], axis=0)

    u2, u2w = stacked_tri(tq), stacked_tri(2 * tq)
    return pl.pallas_call(
        functools.partial(_sb_attn_body, tq=tq, chunk=min(seq, 256)),
        grid=(batch, HEADS, nq),
        in_specs=[
            pl.BlockSpec((tq, LANES), lambda b, h, i: (b * nq + i, h)),
            pl.BlockSpec((seq, LANES), lambda b, h, i: (b, HEADS + h)),
            pl.BlockSpec((seq, LANES), lambda b, h, i: (b, 2 * HEADS + h)),
            pl.BlockSpec((1, LANES), lambda b, h, i: (0, 0)),
            pl.BlockSpec((1, LANES), lambda b, h, i: (0, 0)),
            pl.BlockSpec((2 * tq, tq), lambda b, h, i: (0, 0)),
            pl.BlockSpec((4 * tq, 2 * tq), lambda b, h, i: (0, 0)),
        ],
        out_specs=pl.BlockSpec((tq, LANES), lambda b, h, i: (b * nq + i, h)),
        out_shape=jax.ShapeDtypeStruct((n, SB_WIDTH), BF16),
        scratch_shapes=[pltpu.VMEM((seq, LANES), BF16), pltpu.VMEM((seq, LANES), BF16)],
        compiler_params=_params(("parallel", "parallel", "arbitrary")),
    )(proj, proj, proj, qn_w.reshape(1, LANES), kn_w.reshape(1, LANES), u2, u2w)


def _hgrn_chunk(q, f_logit, val, gate, lbp, nw, l3, state_t):
    c, s = HG_CHUNK, HG_SUB
    log_lb = lbp[0:1, :]
    log_1m_lb = lbp[1:2, :]
    cc = log_1m_lb + _log_sigmoid(f_logit)
    log_f = jnp.maximum(log_lb, cc) + jnp.log(1.0 + jnp.exp(-jnp.abs(log_lb - cc)))
    k = 1.0 - jnp.exp(log_f)

    hi = log_f.astype(BF16)
    r1 = log_f - hi.astype(F32)
    mid = r1.astype(BF16)
    lo = (r1 - mid.astype(F32)).astype(BF16)
    b = jnp.dot(l3, jnp.concatenate([hi, mid, lo], axis=0), preferred_element_type=F32)

    o = lax.dot_general((q * jnp.exp(b)).astype(BF16), state_t.astype(BF16),
                        (((1,), (1,)), ((), ())), preferred_element_type=F32)
    val_bf = val.astype(BF16)

    rows = []
    for blk in range(c // s):
        lo_r = blk * s
        qb, bb = q[lo_r:lo_r + s], b[lo_r:lo_r + s]
        ob = o[lo_r:lo_r + s]
        if blk > 0:
            ref_b = b[lo_r - 1:lo_r]
            qt = (qb * jnp.exp(bb - ref_b)).astype(BF16)
            kt = (k[:lo_r] * jnp.exp(ref_b - b[:lo_r])).astype(BF16)
            sc = lax.dot_general(qt, kt, (((1,), (1,)), ((), ())), preferred_element_type=F32)
            ob = ob + jnp.dot(sc.astype(BF16), val_bf[:lo_r], preferred_element_type=F32)
        trow = lax.broadcasted_iota(jnp.int32, (s, LANES), 0)
        for j in range(s):
            d = jnp.where(trow >= j, bb - bb[j:j + 1], -jnp.inf)
            p = qb * k[lo_r + j:lo_r + j + 1] * jnp.exp(d)
            ob = ob + jnp.sum(p, axis=-1, keepdims=True) * val[lo_r + j:lo_r + j + 1]
        rows.append(ob)
    o = jnp.concatenate(rows, axis=0)

    b_last = b[c - 1:c]
    kd = (k * jnp.exp(b_last - b)).astype(BF16)
    upd = jnp.dot(val.T.astype(BF16), kd, preferred_element_type=F32)
    state_t = state_t * jnp.exp(b_last) + upd

    y = _rms_rows(o, nw) * _sigmoid(gate)
    return y, state_t


def _hgrn_body(q_ref, f_ref, i_ref, g_ref, lbp_ref, nw_ref, l3_ref, o_ref, st_ref):
    @pl.when(pl.program_id(2) == 0)
    def _():
        st_ref[...] = jnp.zeros_like(st_ref)

    n_chunks = q_ref.shape[0] // HG_CHUNK

    def body(ci, state_t):
        r = pl.ds(pl.multiple_of(ci * HG_CHUNK, HG_CHUNK), HG_CHUNK)
        y, state_t = _hgrn_chunk(q_ref[r, :], f_ref[r, :], i_ref[r, :], g_ref[r, :],
                                 lbp_ref[0], nw_ref[...], l3_ref[...], state_t)
        o_ref[r, :] = y.astype(o_ref.dtype)
        return state_t

    st_ref[...] = lax.fori_loop(0, n_chunks, body, st_ref[...], unroll=8)


def hgrn2(proj, lb, nw, *, batch, seq, rows):
    n = proj.shape[0]
    ns = seq // rows
    base = 3 * HEADS
    lbh = lb.reshape(HEADS, 1, LANES).astype(F32)
    lbp = jnp.concatenate([jnp.log(lbh), jnp.log1p(-lbh)], axis=1)
    tri = (jnp.arange(HG_CHUNK)[:, None] >= jnp.arange(HG_CHUNK)[None, :]).astype(BF16)
    l3 = jnp.concatenate([tri, tri, tri], axis=1)

    def col(k):
        return pl.BlockSpec((rows, LANES), lambda b, h, i: (b * ns + i, base + k * HEADS + h))

    return pl.pallas_call(
        _hgrn_body,
        grid=(batch, HEADS, ns),
        in_specs=[
            col(0), col(1), col(2), col(3),
            pl.BlockSpec((1, 2, LANES), lambda b, h, i: (h, 0, 0)),
            pl.BlockSpec((1, LANES), lambda b, h, i: (0, 0)),
            pl.BlockSpec((HG_CHUNK, 3 * HG_CHUNK), lambda b, h, i: (0, 0)),
        ],
        out_specs=pl.BlockSpec((rows, LANES), lambda b, h, i: (b * ns + i, h)),
        out_shape=jax.ShapeDtypeStruct((n, HG_WIDTH), BF16),
        scratch_shapes=[pltpu.VMEM((LANES, LANES), F32)],
        compiler_params=_params(("parallel", "parallel", "arbitrary")),
    )(proj, proj, proj, proj, lbp, nw.reshape(1, LANES), l3)


def _conv_body(a_ref, b_ref, ah_ref, bh_ref, w_ref, cb_ref, lw_ref, lb_ref, o_ref, h_s, c_s, *, rb):
    ts = a_ref.shape[0]
    first = pl.program_id(1) == 0
    halo = ah_ref[...] * _sigmoid(bh_ref[...])
    h_s[0:CONV_HALO, :] = jnp.where(first, 0.0, halo)
    h_s[CONV_HALO:, :] = a_ref[...] * _sigmoid(b_ref[...])
    shift = CONV_HALO - (CONV_TAPS - 1)

    def body(ri, carry):
        r0 = pl.multiple_of(ri * rb, rb)
        for cblk in range(CONV_CH // LANES):
            cs = slice(cblk * LANES, (cblk + 1) * LANES)
            acc = jnp.zeros((rb, LANES), F32) + cb_ref[:, cs]
            window = h_s[pl.ds(r0, rb + CONV_HALO), cs]
            for j in range(CONV_TAPS):
                acc = acc + w_ref[j:j + 1, cs] * window[shift + j:shift + j + rb]
            c_s[pl.ds(r0, rb), cs] = acc
        return carry

    lax.fori_loop(0, ts // rb, body, 0)

    c = c_s[...]
    mu = jnp.mean(c, axis=-1, keepdims=True)
    cen = c - mu
    var = jnp.mean(cen * cen, axis=-1, keepdims=True)
    hn = cen * lax.rsqrt(var + EPS) * lw_ref[...] + lb_ref[...]
    o_ref[...] = (hn * _sigmoid(hn)).astype(o_ref.dtype)


def conformer_conv(proj, conv_w, conv_b, ln_w, ln_b, *, batch, seq, ts):
    n = proj.shape[0]
    ns = seq // ts
    a_blk = (3 * SB_WIDTH + 4 * HG_WIDTH) // CONV_CH
    hpt = ts // CONV_HALO

    def halo_map(cblk):
        return lambda b, i: (jnp.maximum(b * ns * hpt + i * hpt - 1, 0), cblk)

    vec = lambda v: v.reshape(1, CONV_CH)
    return pl.pallas_call(
        functools.partial(_conv_body, rb=32),
        grid=(batch, ns),
        in_specs=[
            pl.BlockSpec((ts, CONV_CH), lambda b, i: (b * ns + i, a_blk)),
            pl.BlockSpec((ts, CONV_CH), lambda b, i: (b * ns + i, a_blk + 1)),
            pl.BlockSpec((CONV_HALO, CONV_CH), halo_map(a_blk)),
            pl.BlockSpec((CONV_HALO, CONV_CH), halo_map(a_blk + 1)),
            pl.BlockSpec((CONV_TAPS, CONV_CH), lambda b, i: (0, 0)),
            pl.BlockSpec((1, CONV_CH), lambda b, i: (0, 0)),
            pl.BlockSpec((1, CONV_CH), lambda b, i: (0, 0)),
            pl.BlockSpec((1, CONV_CH), lambda b, i: (0, 0)),
        ],
        out_specs=pl.BlockSpec((ts, CONV_CH), lambda b, i: (b * ns + i, 0)),
        out_shape=jax.ShapeDtypeStruct((n, CONV_CH), BF16),
        scratch_shapes=[pltpu.VMEM((CONV_HALO + ts, CONV_CH), F32), pltpu.VMEM((ts, CONV_CH), F32)],
        compiler_params=_params(("parallel", "arbitrary")),
    )(proj, proj, proj, proj, conv_w.reshape(CONV_TAPS, CONV_CH), vec(conv_b), vec(ln_w), vec(ln_b))


GATE_BLOCK = 1024


def _merge_body(x_ref, ysb_ref, yhg_ref, yc_ref, g0, g1, g2, g3, g4, g5,
                wsb_ref, whg_ref, wc_ref, wo_ref, o_ref):
    gates = ((g0, g2, g4), (g1, g3, g5))
    acc = x_ref[...]
    for c in range(2):
        cs = slice(c * GATE_BLOCK, (c + 1) * GATE_BLOCK)
        merged = None
        for y_ref, g_ref, w_ref in zip((ysb_ref, yhg_ref, yc_ref), gates[c],
                                       (wsb_ref, whg_ref, wc_ref)):
            term = _sigmoid(g_ref[...]) * jnp.dot(y_ref[...], w_ref[:, cs],
                                                  preferred_element_type=F32)
            merged = term if merged is None else merged + term
        acc = acc + jnp.dot(merged.astype(BF16), wo_ref[cs, :], preferred_element_type=F32)
    o_ref[...] = acc


def merge_branches(x, proj, y_sb, y_hg, y_c, w_sb, w_hg, w_c, w_o, *, tm):
    n, d = x.shape
    assert d == 2 * GATE_BLOCK
    g_blk = (3 * SB_WIDTH + 4 * HG_WIDTH + 2 * CONV_CH) // GATE_BLOCK
    row = lambda width: pl.BlockSpec((tm, width), lambda i: (i, 0))
    gate = lambda k: pl.BlockSpec((tm, GATE_BLOCK), lambda i: (i, g_blk + k))
    whole = lambda a: pl.BlockSpec(a.shape, lambda i: (0, 0), pipeline_mode=pl.Buffered(1))
    return pl.pallas_call(
        _merge_body,
        grid=(n // tm,),
        in_specs=[row(d), row(SB_WIDTH), row(HG_WIDTH), row(CONV_CH)]
                 + [gate(k) for k in range(6)]
                 + [whole(w_sb), whole(w_hg), whole(w_c), whole(w_o)],
        out_specs=row(d),
        out_shape=jax.ShapeDtypeStruct((n, d), F32),
        compiler_params=_params(("parallel",)),
    )(x, y_sb, y_hg, y_c, *([proj] * 6), w_sb, w_hg, w_c, w_o)


def _peer_scores_body(x_ref, nw_ref, wq_ref, k1_ref, k2_ref, hb_ref, s_ref, *, chunk):
    @pl.when(pl.program_id(1) == 0)
    def _():
        def body(c, carry):
            r = pl.ds(pl.multiple_of(c * chunk, chunk), chunk)
            hb_ref[r, :] = _rms_rows(x_ref[r, :], nw_ref[...]).astype(BF16)
            return carry
        lax.fori_loop(0, x_ref.shape[0] // chunk, body, 0)

    q = jnp.dot(hb_ref[...], wq_ref[...], preferred_element_type=F32).astype(BF16)
    nt = (((1,), (1,)), ((), ()))
    s_ref[0] = lax.dot_general(k1_ref[0], q[:, :LANES], nt, preferred_element_type=F32)
    s_ref[1] = lax.dot_general(k2_ref[0], q[:, LANES:], nt, preferred_element_type=F32)


def peer_scores(x, nw, wq, keys1, keys2, *, tm):
    n, d = x.shape
    return pl.pallas_call(
        functools.partial(_peer_scores_body, chunk=min(tm, 256)),
        grid=(n // tm, HEADS),
        in_specs=[
            pl.BlockSpec((tm, d), lambda i, h: (i, 0)),
            pl.BlockSpec((1, d), lambda i, h: (0, 0)),
            pl.BlockSpec((d, 2 * LANES), lambda i, h: (0, h)),
            pl.BlockSpec((1, PEER_NKEYS, LANES), lambda i, h: (h, 0, 0)),
            pl.BlockSpec((1, PEER_NKEYS, LANES), lambda i, h: (h, 0, 0)),
        ],
        out_specs=[
            pl.BlockSpec((tm, d), lambda i, h: (i, 0)),
            pl.BlockSpec((2, PEER_NKEYS, tm), lambda i, h: (h, 0, i)),
        ],
        out_shape=[
            jax.ShapeDtypeStruct((n, d), BF16),
            jax.ShapeDtypeStruct((2 * HEADS, PEER_NKEYS, n), F32),
        ],
        compiler_params=_params(("parallel", "arbitrary")),
    )(x, nw.reshape(1, d), wq, keys1, keys2)


_CANDIDATES = tuple((a, b) for a in range(PEER_TOPK) for b in range(PEER_TOPK)
                    if (a + 1) * (b + 1) <= PEER_TOPK)


def _top16_keys(x):
    nk = x.shape[0]
    kidx = lax.broadcasted_iota(jnp.int32, x.shape, 0)
    vals, idxs = [], []
    for _ in range(PEER_TOPK):
        m = jnp.max(x, axis=0)
        sel = jnp.min(jnp.where(x == m[None], kidx, nk), axis=0)
        x = jnp.where(kidx == sel[None], -jnp.inf, x)
        vals.append(m)
        idxs.append(sel)
    return vals, idxs


def _peer_topk_math(s1, s2):
    v1, i1 = _top16_keys(s1)
    v2, i2 = _top16_keys(s2)
    cand = [v1[a] + v2[b] for a, b in _CANDIDATES]
    expert = [i1[a] * PEER_NKEYS + i2[b] for a, b in _CANDIDATES]
    flat = [a * PEER_TOPK + b for a, b in _CANDIDATES]
    big = PEER_TOPK * PEER_TOPK
    tops, exps = [], []
    for _ in range(PEER_TOPK):
        m = functools.reduce(jnp.maximum, cand)
        sel = functools.reduce(jnp.minimum, [jnp.where(c == m, f, big) for c, f in zip(cand, flat)])
        hit = [sel == f for f in flat]
        exps.append(functools.reduce(jnp.add, [jnp.where(h, e, 0) for h, e in zip(hit, expert)]))
        cand = [jnp.where(h, -jnp.inf, c) for h, c in zip(hit, cand)]
        tops.append(m)
    w = [jnp.exp(t - tops[0]) for t in tops]
    inv = 1.0 / functools.reduce(jnp.add, w)
    return exps, [wk * inv for wk in w]


def _peer_topk_body(s_ref, e_ref, g_ref):
    exps, gates = _peer_topk_math(s_ref[0], s_ref[1])
    for k in range(PEER_TOPK):
        e_ref[0, k] = exps[k]
        g_ref[0, k] = gates[k]


def peer_topk(scores4):
    _, nk, rows, lanes = scores4.shape
    slab = 8
    out_spec = pl.BlockSpec((1, PEER_TOPK, slab, lanes), lambda i, h: (h, 0, i, 0))
    shape = (HEADS, PEER_TOPK, rows, lanes)
    return pl.pallas_call(
        _peer_topk_body,
        grid=(rows // slab, HEADS),
        in_specs=[pl.BlockSpec((2, nk, slab, lanes), lambda i, h: (h, 0, i, 0))],
        out_specs=[out_spec, out_spec],
        out_shape=[jax.ShapeDtypeStruct(shape, jnp.int32), jax.ShapeDtypeStruct(shape, F32)],
        compiler_params=_params(("parallel", "parallel")),
    )(scores4)


W_PITCH = 136


def _peer_gatemat_body(e_ref, g_ref, w_ref, wt_s):
    tb = e_ref.shape[0]
    sub = lax.broadcasted_iota(jnp.int32, (PEER_NKEYS, LANES), 0)

    def build(t, carry):
        e = e_ref[pl.ds(t, 1), :]
        g = g_ref[pl.ds(t, 1), :]
        a_t = jnp.where(sub == (e >> 7), g, 0.0).astype(BF16)
        b_t = jnp.where(sub == (e & 127), 1.0, 0.0).astype(BF16)
        wt = lax.dot_general(a_t, b_t, (((1,), (1,)), ((), ())), preferred_element_type=F32)
        wt_s[pl.ds(pl.multiple_of(t * W_PITCH, 8), PEER_NKEYS), :] = wt
        return carry

    lax.fori_loop(0, tb, build, 0, unroll=8)

    def emit(j, carry):
        rows = wt_s[pl.ds(j, tb, stride=W_PITCH), :]
        w_ref[:, pl.ds(pl.multiple_of(j * LANES, LANES), LANES)] = rows.astype(w_ref.dtype)
        return carry

    lax.fori_loop(0, PEER_NKEYS, emit, 0, unroll=4)


def peer_gate_matrix(experts, gates, *, tb):
    n = experts.shape[0]
    ne = PEER_NKEYS * PEER_NKEYS
    return pl.pallas_call(
        _peer_gatemat_body,
        grid=(n // tb,),
        in_specs=[pl.BlockSpec((tb, LANES), lambda i: (i, 0)),
                  pl.BlockSpec((tb, LANES), lambda i: (i, 0))],
        out_specs=pl.BlockSpec((tb, ne), lambda i: (i, 0)),
        out_shape=jax.ShapeDtypeStruct((n, ne), BF16),
        scratch_shapes=[pltpu.VMEM((tb * W_PITCH, LANES), F32)],
        compiler_params=_params(("parallel",)),
    )(experts, gates)


def _peer_dense_body(x_ref, hb_ref, w_ref, u_ref, v_ref, o_ref):
    @pl.when(pl.program_id(1) == 0)
    def _():
        o_ref[...] = x_ref[...]

    act = lax.dot_general(hb_ref[...], u_ref[...], (((1,), (1,)), ((), ())),
                          preferred_element_type=F32)
    gelu = 0.5 * act * (1.0 + lax.erf(act * (2.0 ** -0.5)))
    p = (w_ref[...].astype(F32) * gelu).astype(BF16)
    o_ref[...] += jnp.dot(p, v_ref[...], preferred_element_type=F32)


def peer_dense(x, hb, wmat, u, v, *, tm, te):
    n, d = x.shape
    ne = u.shape[0]
    return pl.pallas_call(
        _peer_dense_body,
        grid=(n // tm, ne // te),
        in_specs=[
            pl.BlockSpec((tm, d), lambda i, e: (i, 0), pipeline_mode=pl.Buffered(1)),
            pl.BlockSpec((tm, d), lambda i, e: (i, 0)),
            pl.BlockSpec((tm, te), lambda i, e: (i, e)),
            pl.BlockSpec((te, d), lambda i, e: (e, 0)),
            pl.BlockSpec((te, d), lambda i, e: (e, 0)),
        ],
        out_specs=pl.BlockSpec((tm, d), lambda i, e: (i, 0)),
        out_shape=jax.ShapeDtypeStruct((n, d), F32),
        compiler_params=_params(("parallel", "arbitrary")),
    )(x, hb, wmat, u, v)


def _tile(n, want):
    return min(n, want)


def mixer_layer(x2, batch, seq, norm_w, w_in, qn_w, kn_w, lb, hg_nw, conv_w, conv_b, ln_w, ln_b,
                w_sb, w_hg, w_c, w_o):
    n = x2.shape[0]
    proj = norm_matmul(x2, norm_w, w_in.astype(BF16), tm=_tile(n, 1024), tn=1024)
    y_sb = sb_attention(proj, qn_w, kn_w, batch=batch, seq=seq, tq=_tile(seq, 256))
    y_hg = hgrn2(proj, lb, hg_nw, batch=batch, seq=seq, rows=_tile(seq, 512))
    y_c = conformer_conv(proj, conv_w, conv_b, ln_w, ln_b, batch=batch, seq=seq, ts=_tile(seq, 256))
    return merge_branches(x2, proj, y_sb, y_hg, y_c, w_sb.astype(BF16), w_hg.astype(BF16),
                          w_c.astype(BF16), w_o.astype(BF16), tm=_tile(n, 256))


def peer_layer(x2, norm_w, w_query, keys1, keys2, u, v):
    n = x2.shape[0]
    hb, scores = peer_scores(x2, norm_w, w_query.astype(BF16), keys1.astype(BF16),
                             keys2.astype(BF16), tm=_tile(n, 1024))
    scores4 = scores.reshape(2 * HEADS, PEER_NKEYS, n // LANES, LANES)
    experts, gates = peer_topk(scores4)
    to_rows = lambda a: a.reshape(HEADS * PEER_TOPK, n).T
    wmat = peer_gate_matrix(to_rows(experts), to_rows(gates), tb=_tile(n, 128))
    return peer_dense(x2, hb, wmat, u.astype(BF16), v.astype(BF16), tm=_tile(n, 1024), te=512)


def _lower_bounds(p):
    c = jnp.cumsum(jax.nn.softmax(p.astype(F32), axis=0), axis=0)
    return c - c[0:1]


def kernel(x, attn_norm_w, w_in, sb_q_norm_w, sb_k_norm_w, hg_lower_bounds, hg_norm_w, conv_w, conv_b, conv_ln_w, conv_ln_b, w_br_sb, w_br_hg, w_br_conv, w_o, ffn_norm_w, peer_w_query, peer_keys1, peer_keys2, peer_u, peer_v):
    batch, seq, d = x.shape
    lbs = _lower_bounds(hg_lower_bounds)
    x2 = x.reshape(batch * seq, d)
    for l in range(w_in.shape[0]):
        x2 = mixer_layer(x2, batch, seq, attn_norm_w[l], w_in[l], sb_q_norm_w[l], sb_k_norm_w[l],
                         lbs[l], hg_norm_w[l], conv_w[l], conv_b[l], conv_ln_w[l], conv_ln_b[l],
                         w_br_sb[l], w_br_hg[l], w_br_conv[l], w_o[l])
        x2 = peer_layer(x2, ffn_norm_w[l], peer_w_query[l], peer_keys1[l], peer_keys2[l],
                        peer_u[l], peer_v[l])
    return x2.reshape(batch, seq, d)
```
